```python
import jax, jax.numpy as jnp
from jax import lax
import numpy as np

D_MODEL = 2048
BATCH = 4
SEQ = 4096
DEPTH = 2

MLA_HEADS = 16
MLA_NOPE = 128
MLA_ROPE = 64
MLA_QK = MLA_NOPE + MLA_ROPE
MLA_V = 128
Q_LORA = 512
KV_LORA = 512
ROPE_THETA = 10000.0
SB_HEADS = 16
SB_DIM = 64
FOX_HEADS = 16
FOX_DIM = 64
BLOCK_Q = 128
N_BRANCH = 3
EPS = 1e-6
N_GROUPS = 8
EXPERTS_PER_GROUP = 8
N_EXPERTS = N_GROUPS * EXPERTS_PER_GROUP
TOP_K = 2
D_EXPERT = 512
MOE_BLOCK = 128
D_IN = (Q_LORA + KV_LORA + MLA_ROPE + 3 * SB_HEADS * SB_DIM + 3 * FOX_HEADS * FOX_DIM
        + FOX_HEADS + N_BRANCH * D_MODEL)

kernel_name = "hybrid_mla_stickbreak_fox_hmoe"


def _segment_offsets():
    segs = (("mla_cq", Q_LORA), ("mla_ckv", KV_LORA), ("mla_krope", MLA_ROPE),
            ("sb_q", SB_HEADS * SB_DIM), ("sb_k", SB_HEADS * SB_DIM), ("sb_v", SB_HEADS * SB_DIM),
            ("fox_q", FOX_HEADS * FOX_DIM), ("fox_k", FOX_HEADS * FOX_DIM), ("fox_v", FOX_HEADS * FOX_DIM),
            ("fox_f", FOX_HEADS), ("gates", N_BRANCH * D_MODEL))
    off, start = {}, 0
    for name, width in segs:
        off[name] = (start, start + width)
        start += width
    return off


def _proj(xn, w_in_l, off, name):
    a, b = off[name]
    return xn @ w_in_l[:, a:b]


def _rms(x, g):
    xf = x.astype(jnp.float32)
    y = xf * lax.rsqrt(jnp.mean(xf * xf, axis=-1, keepdims=True) + EPS)
    return (y * g.astype(jnp.float32)).astype(x.dtype)


def _rope_tables(positions):
    inv_freq = ROPE_THETA ** (-jnp.arange(0, MLA_ROPE, 2, dtype=jnp.float32) / MLA_ROPE)
    ang = positions.astype(jnp.float32)[..., None] * inv_freq
    return jnp.cos(ang), jnp.sin(ang)


def _rope(x, cos, sin):
    half = x.shape[-1] // 2
    xf = x.astype(jnp.float32)
    x1, x2 = xf[..., :half], xf[..., half:]
    return jnp.concatenate([x1 * cos - x2 * sin, x2 * cos + x1 * sin], axis=-1).astype(x.dtype)


def _softmax_attention(q, k, v, scale, log_decay=None):
    S = q.shape[1]
    outs = []
    for i in range(S // BLOCK_Q):
        lo, hi = i * BLOCK_Q, (i + 1) * BLOCK_Q
        s = jnp.einsum("bqhd,bkhd->bhqk", q[:, lo:hi], k[:, :hi]).astype(jnp.float32) * scale
        if log_decay is not None:
            s = s + (log_decay[:, :, lo:hi, None] - log_decay[:, :, None, :hi])
        mask = (lo + jnp.arange(BLOCK_Q))[:, None] >= jnp.arange(hi)[None, :]
        p = jax.nn.softmax(jnp.where(mask, s, -jnp.inf), axis=-1)
        outs.append(jnp.einsum("bhqk,bkhd->bqhd", p.astype(v.dtype), v[:, :hi]))
    return jnp.concatenate(outs, axis=1)


def _stick_breaking_attention(q, k, v, scale):
    S = q.shape[1]
    outs = []
    for i in range(S // BLOCK_Q):
        lo, hi = i * BLOCK_Q, (i + 1) * BLOCK_Q
        z = jnp.einsum("bqhd,bkhd->bhqk", q[:, lo:hi], k[:, :hi]).astype(jnp.float32) * scale
        strict = (lo + jnp.arange(BLOCK_Q))[:, None] > jnp.arange(hi)[None, :]
        log_keep = jnp.where(strict, jax.nn.log_sigmoid(-z), 0.0)
        suffix = lax.cumsum(log_keep, axis=3, reverse=True) - log_keep
        w = jnp.where(strict, jnp.exp(jax.nn.log_sigmoid(z) + suffix), 0.0)
        outs.append(jnp.einsum("bhqk,bkhd->bqhd", w.astype(v.dtype), v[:, :hi]))
    return jnp.concatenate(outs, axis=1)


def _mla_branch(xn, w_in_l, off, g_cq, w_uq, g_ckv, w_ukv, g_q, g_k, cos, sin):
    B, S, _ = xn.shape
    c_q = _rms(_proj(xn, w_in_l, off, "mla_cq"), g_cq)
    q = (c_q @ w_uq).reshape(B, S, MLA_HEADS, MLA_QK)
    c_kv = _rms(_proj(xn, w_in_l, off, "mla_ckv"), g_ckv)
    kv = (c_kv @ w_ukv).reshape(B, S, MLA_HEADS, MLA_NOPE + MLA_V)
    k_nope, v = kv[..., :MLA_NOPE], kv[..., MLA_NOPE:]
    k_rope = _proj(xn, w_in_l, off, "mla_krope")
    k_rope = jnp.broadcast_to(k_rope[:, :, None, :], (B, S, MLA_HEADS, MLA_ROPE))
    k = jnp.concatenate([k_nope, k_rope], axis=-1)
    q, k = _rms(q, g_q), _rms(k, g_k)
    c, s = cos[:, :, None, :], sin[:, :, None, :]
    q = jnp.concatenate([q[..., :MLA_NOPE], _rope(q[..., MLA_NOPE:], c, s)], axis=-1)
    k = jnp.concatenate([k[..., :MLA_NOPE], _rope(k[..., MLA_NOPE:], c, s)], axis=-1)
    o = _softmax_attention(q, k, v, MLA_QK ** -0.5)
    return o.reshape(B, S, MLA_HEADS * MLA_V)


def _sb_branch(xn, w_in_l, off, g_q, g_k):
    B, S, _ = xn.shape
    shp = (B, S, SB_HEADS, SB_DIM)
    q = _rms(_proj(xn, w_in_l, off, "sb_q").reshape(shp), g_q)
    k = _rms(_proj(xn, w_in_l, off, "sb_k").reshape(shp), g_k)
    v = _proj(xn, w_in_l, off, "sb_v").reshape(shp)
    o = _stick_breaking_attention(q, k, v, SB_DIM ** -0.5)
    return o.reshape(B, S, SB_HEADS * SB_DIM)


def _fox_branch(xn, w_in_l, off, g_q, g_k, f_bias):
    B, S, _ = xn.shape
    shp = (B, S, FOX_HEADS, FOX_DIM)
    q = _rms(_proj(xn, w_in_l, off, "fox_q").reshape(shp), g_q)
    k = _rms(_proj(xn, w_in_l, off, "fox_k").reshape(shp), g_k)
    v = _proj(xn, w_in_l, off, "fox_v").reshape(shp)
    f_logit = _proj(xn, w_in_l, off, "fox_f").astype(jnp.float32) + f_bias.astype(jnp.float32)
    cum_log_f = jnp.cumsum(jax.nn.log_sigmoid(f_logit), axis=1)
    o = _softmax_attention(q, k, v, FOX_DIM ** -0.5, jnp.transpose(cum_log_f, (0, 2, 1)))
    return o.reshape(B, S, FOX_HEADS * FOX_DIM)


def _hier_moe(h, w_group, w_router, w_gate, w_up, w_down):
    N, D = h.shape
    rows = jnp.arange(N)
    g_logits = (h @ w_group).astype(jnp.float32)
    g_prob = jax.nn.softmax(g_logits, axis=-1)
    g_idx = jnp.argmax(g_logits, axis=-1)
    p_group = g_prob[rows, g_idx]
    e_logits = (h @ w_router).astype(jnp.float32).reshape(N, N_GROUPS, EXPERTS_PER_GROUP)
    e_in = e_logits[rows, g_idx]
    top_v, top_i = lax.top_k(e_in, TOP_K)
    gate = p_group[:, None] * jax.nn.softmax(top_v, axis=-1)
    expert = g_idx[:, None] * EXPERTS_PER_GROUP + top_i
    A = N * TOP_K
    e_flat = expert.reshape(-1)
    tok_flat = jnp.repeat(rows, TOP_K)
    order = jnp.argsort(e_flat)
    e_s, tok_s, w_s = e_flat[order], tok_flat[order], gate.reshape(-1)[order]
    counts = jnp.bincount(e_flat, length=N_EXPERTS)
    starts = jnp.cumsum(counts) - counts
    padded = ((counts + MOE_BLOCK - 1) // MOE_BLOCK) * MOE_BLOCK
    pad_ends = jnp.cumsum(padded)
    pad_starts = pad_ends - padded
    dest = pad_starts[e_s] + (jnp.arange(A) - starts[e_s])
    n_blocks = (A + N_EXPERTS * (MOE_BLOCK - 1)) // MOE_BLOCK + 1
    P = n_blocks * MOE_BLOCK
    buf_tok = jnp.zeros((P,), jnp.int32).at[dest].set(tok_s.astype(jnp.int32))
    buf_w = jnp.zeros((P,), h.dtype).at[dest].set(w_s.astype(h.dtype))
    block_e = jnp.minimum(
        jnp.searchsorted(pad_ends, jnp.arange(n_blocks) * MOE_BLOCK, side="right"), N_EXPERTS - 1)

    def expert_block(args):
        tok, e = args
        xb = h[tok]
        return (jax.nn.silu(xb @ w_gate[e]) * (xb @ w_up[e])) @ w_down[e]

    y = lax.map(expert_block, (buf_tok.reshape(n_blocks, MOE_BLOCK), block_e))
    y = y.reshape(P, D) * buf_w[:, None]
    return jnp.zeros((N, D), h.dtype).at[buf_tok].add(y)


def setup_inputs(seed: int = 0) -> dict:
    key = jax.random.key(seed)
    k = jax.random.split(key, 32)
    f32 = jnp.float32
    L = DEPTH

    def nrm(i, shape, fan_in):
        return jax.random.normal(k[i], shape, f32) * (fan_in ** -0.5)

    def gain(i, shape):
        return 1.0 + 0.02 * jax.random.normal(k[i], shape, f32)

    x = jax.random.normal(k[0], (BATCH, SEQ, D_MODEL), f32)
    offs = jax.random.randint(k[1], (BATCH, 1), 0, 1024, dtype=jnp.int32)
    positions = (offs + jnp.arange(SEQ, dtype=jnp.int32)[None, :]).astype(jnp.int32)
    return {
        "x": x,
        "positions": positions,
        "norm_mix": gain(2, (L, D_MODEL)),
        "w_in": nrm(3, (L, D_MODEL, D_IN), D_MODEL),
        "mla_cq_norm": gain(4, (L, Q_LORA)),
        "mla_w_uq": nrm(5, (L, Q_LORA, MLA_HEADS * MLA_QK), Q_LORA),
        "mla_ckv_norm": gain(6, (L, KV_LORA)),
        "mla_w_ukv": nrm(7, (L, KV_LORA, MLA_HEADS * (MLA_NOPE + MLA_V)), KV_LORA),
        "mla_q_norm": gain(8, (L, MLA_QK)),
        "mla_k_norm": gain(9, (L, MLA_QK)),
        "sb_q_norm": gain(10, (L, SB_DIM)),
        "sb_k_norm": gain(11, (L, SB_DIM)),
        "fox_q_norm": gain(12, (L, FOX_DIM)),
        "fox_k_norm": gain(13, (L, FOX_DIM)),
        "fox_f_bias": 3.0 + 0.5 * jax.random.normal(k[14], (L, FOX_HEADS), f32),
        "w_branch_mla": nrm(15, (L, MLA_HEADS * MLA_V, D_MODEL), MLA_HEADS * MLA_V),
        "w_branch_sb": nrm(16, (L, SB_HEADS * SB_DIM, D_MODEL), SB_HEADS * SB_DIM),
        "w_branch_fox": nrm(17, (L, FOX_HEADS * FOX_DIM, D_MODEL), FOX_HEADS * FOX_DIM),
        "w_out": nrm(18, (L, D_MODEL, D_MODEL), D_MODEL),
        "norm_ffn": gain(19, (L, D_MODEL)),
        "w_group": nrm(20, (L, D_MODEL, N_GROUPS), D_MODEL),
        "w_router": nrm(21, (L, D_MODEL, N_EXPERTS), D_MODEL),
        "w_e_gate": nrm(22, (L, N_EXPERTS, D_MODEL, D_EXPERT), D_MODEL),
        "w_e_up": nrm(23, (L, N_EXPERTS, D_MODEL, D_EXPERT), D_MODEL),
        "w_e_down": nrm(24, (L, N_EXPERTS, D_EXPERT, D_MODEL), D_EXPERT),
    }


def reference(x, positions, norm_mix, w_in, mla_cq_norm, mla_w_uq, mla_ckv_norm, mla_w_ukv,
              mla_q_norm, mla_k_norm, sb_q_norm, sb_k_norm, fox_q_norm, fox_k_norm, fox_f_bias,
              w_branch_mla, w_branch_sb, w_branch_fox, w_out, norm_ffn, w_group, w_router,
              w_e_gate, w_e_up, w_e_down):
    B, S, D = x.shape
    cos, sin = _rope_tables(positions)
    off = _segment_offsets()
    for l in range(DEPTH):
        xn = _rms(x, norm_mix[l])
        w_in_l = w_in[l]
        o_a = _mla_branch(xn, w_in_l, off, mla_cq_norm[l], mla_w_uq[l], mla_ckv_norm[l],
                          mla_w_ukv[l], mla_q_norm[l], mla_k_norm[l], cos, sin)
        o_b = _sb_branch(xn, w_in_l, off, sb_q_norm[l], sb_k_norm[l])
        o_c = _fox_branch(xn, w_in_l, off, fox_q_norm[l], fox_k_norm[l], fox_f_bias[l])
        gates = jax.nn.sigmoid(_proj(xn, w_in_l, off, "gates").astype(jnp.float32))
        gates = gates.reshape(B, S, N_BRANCH, D).astype(x.dtype)
        merged = (gates[:, :, 0] * (o_a @ w_branch_mla[l])
                  + gates[:, :, 1] * (o_b @ w_branch_sb[l])
                  + gates[:, :, 2] * (o_c @ w_branch_fox[l]))
        x = x + merged @ w_out[l]
        h = _rms(x, norm_ffn[l]).reshape(B * S, D)
        x = x + _hier_moe(h, w_group[l], w_router[l], w_e_gate[l], w_e_up[l], w_e_down[l]).reshape(B, S, D)
    return x
```

```python
import functools

import jax
import jax.numpy as jnp
import numpy as np
from jax import lax
from jax.experimental import pallas as pl
from jax.experimental.pallas import tpu as pltpu

F32 = jnp.float32
BF16 = jnp.bfloat16
U32 = jnp.uint32
I32 = jnp.int32
SDS = jax.ShapeDtypeStruct

D_MODEL = 2048
MLA_HEADS = 16
MLA_NOPE = 128
MLA_ROPE = 64
MLA_QK = MLA_NOPE + MLA_ROPE
MLA_V = 128
Q_LORA = 512
KV_LORA = 512
ROPE_THETA = 10000.0
SB_HEADS = 16
SB_DIM = 64
FOX_HEADS = 16
FOX_DIM = 64
N_BRANCH = 3
EPS = 1e-6
N_GROUPS = 8
EXPERTS_PER_GROUP = 8
N_EXPERTS = N_GROUPS * EXPERTS_PER_GROUP
TOP_K = 2
D_EXPERT = 512

LANES = 128
SUBLANES = 8
VMEM_LIMIT_BYTES = 56 * 1024 * 1024

NEG_BIG = -1e30
HALF = LANES // 2
ROW_SLABS = D_MODEL // 2 // LANES
MOE_BLOCK = 256

COL_CQ = 0
COL_CKV = 512
COL_SB_Q = 1024
COL_SB_K = 2048
COL_SB_V = 3072
COL_FOX_Q = 4096
COL_FOX_K = 5120
COL_FOX_V = 6144
COL_GATES = 7168
COL_SPECIAL = COL_GATES + N_BRANCH * D_MODEL
N_PACKED = COL_SPECIAL + 512


def _params(sem):
    return pltpu.CompilerParams(dimension_semantics=sem, vmem_limit_bytes=VMEM_LIMIT_BYTES)


def _rms_rows(x, g):
    ms = jnp.mean(x * x, axis=-1, keepdims=True)
    return x * lax.rsqrt(ms + EPS) * g


def _split3(x):
    a1 = x.astype(BF16)
    r1 = x - a1.astype(F32)
    a2 = r1.astype(BF16)
    r2 = r1 - a2.astype(F32)
    return a1, a2, r2.astype(BF16)


def _split2(x):
    a1 = x.astype(BF16)
    return a1, (x - a1.astype(F32)).astype(BF16)


def _lane_tile(x, reps):
    return jnp.concatenate([x] * reps, axis=1) if reps > 1 else x


def _inproj_kernel(x_ref, g_ref, w_ref, o_ref, f_ref, xn_ref, *, special_j):
    j = pl.program_id(1)

    @pl.when(j == 0)
    def _():
        xn_ref[...] = _rms_rows(x_ref[...], g_ref[...]).astype(BF16)

    acc = jnp.dot(xn_ref[...], w_ref[...], preferred_element_type=F32)
    o_ref[...] = acc.astype(BF16)

    @pl.when(j == special_j)
    def _():
        f_ref[...] = acc


def _inproj(x2, g, w, tm, tn):
    n, d = x2.shape
    npk = w.shape[1]
    return pl.pallas_call(
        functools.partial(_inproj_kernel, special_j=COL_SPECIAL // tn),
        grid=(n // tm, npk // tn),
        in_specs=[
            pl.BlockSpec((tm, d), lambda i, j: (i, 0)),
            pl.BlockSpec((1, d), lambda i, j: (0, 0)),
            pl.BlockSpec((d, tn), lambda i, j: (0, j)),
        ],
        out_specs=[
            pl.BlockSpec((tm, tn), lambda i, j: (i, j)),
            pl.BlockSpec((tm, tn), lambda i, j: (i, 0)),
        ],
        out_shape=[SDS((n, npk), BF16), SDS((n, tn), F32)],
        scratch_shapes=[pltpu.VMEM((tm, d), BF16)],
        compiler_params=_params(("arbitrary", "arbitrary")),
        name="inproj",
    )(x2, g, w)


def _mla_prep_kernel(cq_ref, ckv_ref, f_ref, gcq_ref, gckv_ref, wq_ref, wkv_ref, gains_ref, c2_ref, s2_ref,
                     q_ref, k_ref, v_ref, cqn_ref, ckvn_ref, *, scale):
    p = pl.program_id(1)

    @pl.when(p == 0)
    def _():
        cqn_ref[...] = _rms_rows(cq_ref[...].astype(F32), gcq_ref[...]).astype(BF16)
        ckvn_ref[...] = _rms_rows(ckv_ref[...].astype(F32), gckv_ref[...]).astype(BF16)

    tm = cq_ref.shape[0]
    lo = lax.broadcasted_iota(I32, (tm, LANES), 1) < HALF
    c2 = c2_ref[...]
    s2 = s2_ref[...]
    gains = gains_ref[...]
    inv_d = 1.0 / MLA_QK

    def ssum(x):
        return jnp.sum(x, axis=-1, keepdims=True)

    def emit(out_ref, n0, n1, rope, ss_rope_lo, ss_rope_hi, g_nope, mult):
        inv0 = lax.rsqrt((ssum(n0 * n0) + ss_rope_lo) * inv_d + EPS) * mult
        inv1 = lax.rsqrt((ssum(n1 * n1) + ss_rope_hi) * inv_d + EPS) * mult
        out_ref[:, 0 * LANES:1 * LANES] = (n0 * inv0 * g_nope).astype(BF16)
        out_ref[:, 1 * LANES:2 * LANES] = jnp.where(lo, rope * inv0, 0.0).astype(BF16)
        out_ref[:, 2 * LANES:3 * LANES] = (n1 * inv1 * g_nope).astype(BF16)
        out_ref[:, 3 * LANES:4 * LANES] = jnp.where(lo, 0.0, rope * inv1).astype(BF16)

    qq = jnp.dot(cqn_ref[...], wq_ref[0], preferred_element_type=F32)
    r = qq[:, 2 * LANES:3 * LANES]
    sw = qq[:, 3 * LANES:4 * LANES]
    r2 = r * r
    rope_q = r * gains[1:2, :] * c2 + sw * gains[2:3, :] * s2
    emit(q_ref, qq[:, 0:LANES], qq[:, LANES:2 * LANES], rope_q,
         ssum(jnp.where(lo, r2, 0.0)), ssum(jnp.where(lo, 0.0, r2)), gains[0:1, :], scale)

    kv = jnp.dot(ckvn_ref[...], wkv_ref[...], preferred_element_type=F32)
    a = f_ref[:, 0:LANES]
    b = f_ref[:, LANES:2 * LANES]
    ss_kr = ssum(jnp.where(lo, a * a, 0.0))
    rope_k = a * gains[4:5, :] * c2 + b * gains[5:6, :] * s2
    emit(k_ref, kv[:, 0:LANES], kv[:, 2 * LANES:3 * LANES], rope_k, ss_kr, ss_kr, gains[3:4, :], 1.0)
    v_ref[:, 0:LANES] = kv[:, LANES:2 * LANES].astype(BF16)
    v_ref[:, LANES:2 * LANES] = kv[:, 3 * LANES:4 * LANES].astype(BF16)


def _mla_prep(pk, fsp, gcq, gckv, wq_pairs, wkv, gains, c2, s2, tm):
    n = pk.shape[0]
    npairs = MLA_HEADS // 2
    return pl.pallas_call(
        functools.partial(_mla_prep_kernel, scale=MLA_QK ** -0.5),
        grid=(n // tm, npairs),
        in_specs=[
            pl.BlockSpec((tm, Q_LORA), lambda i, p: (i, COL_CQ // Q_LORA)),
            pl.BlockSpec((tm, KV_LORA), lambda i, p: (i, COL_CKV // KV_LORA)),
            pl.BlockSpec((tm, 2 * LANES), lambda i, p: (i, 0)),
            pl.BlockSpec((1, Q_LORA), lambda i, p: (0, 0)),
            pl.BlockSpec((1, KV_LORA), lambda i, p: (0, 0)),
            pl.BlockSpec((1, Q_LORA, 4 * LANES), lambda i, p: (p, 0, 0)),
            pl.BlockSpec((KV_LORA, 4 * LANES), lambda i, p: (0, p)),
            pl.BlockSpec((SUBLANES, LANES), lambda i, p: (0, 0)),
            pl.BlockSpec((tm, LANES), lambda i, p: (i, 0)),
            pl.BlockSpec((tm, LANES), lambda i, p: (i, 0)),
        ],
        out_specs=[
            pl.BlockSpec((tm, 4 * LANES), lambda i, p: (i, p)),
            pl.BlockSpec((tm, 4 * LANES), lambda i, p: (i, p)),
            pl.BlockSpec((tm, 2 * LANES), lambda i, p: (i, p)),
        ],
        out_shape=[SDS((n, MLA_HEADS * 2 * LANES), BF16), SDS((n, MLA_HEADS * 2 * LANES), BF16),
                   SDS((n, MLA_HEADS * MLA_V), BF16)],
        scratch_shapes=[pltpu.VMEM((tm, Q_LORA), BF16), pltpu.VMEM((tm, KV_LORA), BF16)],
        compiler_params=_params(("arbitrary", "arbitrary")),
        name="mla_prep",
    )(pk, pk, fsp, gcq, gckv, wq_pairs, wkv, gains, c2, s2)


def _fox_cumsum_kernel(f_ref, bias_ref, tri_ref, cf_ref, carry_ref):
    @pl.when(pl.program_id(1) == 0)
    def _():
        carry_ref[...] = jnp.zeros_like(carry_ref)

    f = f_ref[...] + bias_ref[...]
    ls = jnp.minimum(f, 0.0) - jnp.log1p(jnp.exp(-jnp.abs(f)))
    tri = tri_ref[...]
    cum = carry_ref[0:1, :]
    for piece in _split3(ls):
        cum = cum + jnp.dot(tri, piece, preferred_element_type=F32)
    cf_ref[...] = cum
    ts = f_ref.shape[0]
    carry_ref[...] = jnp.broadcast_to(cum[ts - 1:ts, :], carry_ref.shape)


def _fox_cumsum(fsp, bias_row, batch, seq, ts):
    n = fsp.shape[0]
    tri = (np.arange(ts)[:, None] >= np.arange(ts)[None, :]).astype(np.float32)
    nt = seq // ts
    return pl.pallas_call(
        _fox_cumsum_kernel,
        grid=(batch, nt),
        in_specs=[
            pl.BlockSpec((ts, LANES), lambda b, t: (b * nt + t, 2)),
            pl.BlockSpec((1, LANES), lambda b, t: (0, 0)),
            pl.BlockSpec((ts, ts), lambda b, t: (0, 0)),
        ],
        out_specs=pl.BlockSpec((ts, LANES), lambda b, t: (b * nt + t, 0)),
        out_shape=SDS((n, LANES), F32),
        scratch_shapes=[pltpu.VMEM((SUBLANES, LANES), F32)],
        compiler_params=_params(("arbitrary", "arbitrary")),
        name="fox_cumsum",
    )(fsp, bias_row, jnp.asarray(tri, BF16))


def _head_norm_kernel(*refs, aug_sign):
    if aug_sign is None:
        x_ref, gain_ref, ind_ref, indt_ref, o_ref = refs
    else:
        x_ref, gain_ref, ind_ref, indt_ref, cf_ref, place_ref, const_ref, mask_ref, o_ref = refs
    x = x_ref[...].astype(F32)
    ind = ind_ref[...]
    indt = indt_ref[...]
    sq_hi, sq_lo = _split2(x * x)
    ss = jnp.dot(sq_hi, ind, preferred_element_type=F32) + jnp.dot(sq_lo, ind, preferred_element_type=F32)
    inv = lax.rsqrt(ss * (1.0 / SB_DIM) + EPS)
    inv_hi, inv_lo = _split2(inv)
    inv_b = jnp.dot(inv_hi, indt, preferred_element_type=F32) + jnp.dot(inv_lo, indt, preferred_element_type=F32)
    y = x * inv_b * gain_ref[...]
    if aug_sign is None:
        o_ref[...] = y.astype(BF16)
        return
    pieces = jnp.concatenate(_split3(cf_ref[...] * aug_sign), axis=1)
    aug = jnp.dot(pieces, place_ref[...], preferred_element_type=F32) + const_ref[...]
    npair = y.shape[1] // LANES
    yy = jnp.concatenate([y[:, (c // 2) * LANES:(c // 2 + 1) * LANES] for c in range(2 * npair)], axis=1)
    o_ref[...] = jnp.where(mask_ref[...] > 0.5, yy, aug).astype(BF16)


def _head_indicators(nheads, dim):
    ind = np.zeros((nheads * dim, LANES), np.float32)
    ind[np.arange(nheads * dim), np.arange(nheads * dim) // dim] = 1.0
    return jnp.asarray(ind, BF16), jnp.asarray(ind.T.copy(), BF16)


def _fox_aug_tables(is_query):
    place = np.zeros((3 * LANES, FOX_HEADS * LANES), np.float32)
    const = np.zeros((1, FOX_HEADS * LANES), np.float32)
    mask = np.zeros((1, FOX_HEADS * LANES), np.float32)
    for h in range(FOX_HEADS):
        data0 = h * LANES + (0 if h % 2 == 0 else HALF)
        aug0 = h * LANES + (HALF if h % 2 == 0 else 0)
        mask[0, data0:data0 + HALF] = 1.0
        for i in range(3):
            piece_col = aug0 + i if is_query else aug0 + 3 + i
            ones_col = aug0 + 3 + i if is_query else aug0 + i
            place[i * LANES + h, piece_col] = 1.0
            const[0, ones_col] = 1.0
    return jnp.asarray(place, BF16), jnp.asarray(const, F32), jnp.asarray(mask, F32)


def _head_norm(pk, col, gain_row, tm, cf=None, is_query=True):
    n = pk.shape[0]
    width = SB_HEADS * SB_DIM
    ind, indt = _head_indicators(SB_HEADS, SB_DIM)
    in_specs = [
        pl.BlockSpec((tm, width), lambda i: (i, col // width)),
        pl.BlockSpec((1, width), lambda i: (0, 0)),
        pl.BlockSpec((width, LANES), lambda i: (0, 0)),
        pl.BlockSpec((LANES, width), lambda i: (0, 0)),
    ]
    args = [pk, gain_row, ind, indt]
    out_w = width
    aug_sign = None
    if cf is not None:
        place, const, mask = _fox_aug_tables(is_query)
        out_w = FOX_HEADS * LANES
        aug_sign = 1.0 if is_query else -1.0
        in_specs += [
            pl.BlockSpec((tm, LANES), lambda i: (i, 0)),
            pl.BlockSpec((3 * LANES, out_w), lambda i: (0, 0)),
            pl.BlockSpec((1, out_w), lambda i: (0, 0)),
            pl.BlockSpec((1, out_w), lambda i: (0, 0)),
        ]
        args += [cf, place, const, mask]
    return pl.pallas_call(
        functools.partial(_head_norm_kernel, aug_sign=aug_sign),
        grid=(n // tm,),
        in_specs=in_specs,
        out_specs=pl.BlockSpec((tm, out_w), lambda i: (i, 0)),
        out_shape=SDS((n, out_w), BF16),
        compiler_params=_params(("arbitrary",)),
        name="head_norm",
    )(*args)


def _flash_kernel(q_ref, k_ref, v_ref, o_ref, m_ref, l_ref, acc_ref, *, hp, dkp, tq, tk):
    qi = pl.program_id(2)
    m_ref[...] = jnp.full(m_ref.shape, NEG_BIG, F32)
    l_ref[...] = jnp.zeros(l_ref.shape, F32)
    acc_ref[...] = jnp.zeros(acc_ref.shape, F32)
    nrep = tk // LANES

    def block(j, masked):
        koff = pl.multiple_of(j * tk, tk)
        v = v_ref[pl.ds(koff, tk), :]
        for i in range(hp):
            q = q_ref[:, i * dkp:(i + 1) * dkp]
            k = k_ref[pl.ds(koff, tk), i * dkp:(i + 1) * dkp]
            s = lax.dot_general(q, k, (((1,), (1,)), ((), ())), preferred_element_type=F32)
            if masked:
                row = qi * tq + lax.broadcasted_iota(I32, (tq, tk), 0)
                col = j * tk + lax.broadcasted_iota(I32, (tq, tk), 1)
                s = jnp.where(row >= col, s, NEG_BIG)
            m_prev = m_ref[i]
            m_new = jnp.maximum(m_prev, jnp.max(s, axis=-1, keepdims=True))
            alpha = jnp.exp(m_prev - m_new)
            p = jnp.exp(s - _lane_tile(m_new, nrep))
            l_ref[i] = alpha * l_ref[i] + jnp.sum(p, axis=-1, keepdims=True)
            m_ref[i] = m_new
            acc_ref[i] = alpha * acc_ref[i] + jnp.dot(p.astype(BF16), v, preferred_element_type=F32)

    nfull = qi * (tq // tk)

    def body(j, c):
        block(j, False)
        return c

    lax.fori_loop(0, nfull, body, 0)
    for d in range(tq // tk):
        block(nfull + d, True)

    if hp == 1:
        o_ref[...] = (acc_ref[0] / l_ref[0]).astype(o_ref.dtype)
    else:
        lo = lax.broadcasted_iota(I32, (tq, LANES), 1) < HALF
        o_ref[...] = jnp.where(lo, acc_ref[0] / l_ref[0], acc_ref[1] / l_ref[1]).astype(o_ref.dtype)


def _flash(q, k, v, v_col, batch, seq, hp, dkp, tq, tk):
    n = q.shape[0]
    groups = q.shape[1] // (hp * dkp)
    nq = seq // tq
    vb = v_col // LANES
    return pl.pallas_call(
        functools.partial(_flash_kernel, hp=hp, dkp=dkp, tq=tq, tk=tk),
        grid=(batch, groups, nq),
        in_specs=[
            pl.BlockSpec((tq, hp * dkp), lambda b, g, i: (b * nq + i, g)),
            pl.BlockSpec((seq, hp * dkp), lambda b, g, i: (b, g)),
            pl.BlockSpec((seq, LANES), lambda b, g, i: (b, vb + g)),
        ],
        out_specs=pl.BlockSpec((tq, LANES), lambda b, g, i: (b * nq + i, g)),
        out_shape=SDS((n, groups * LANES), BF16),
        scratch_shapes=[pltpu.VMEM((hp, tq, LANES), F32), pltpu.VMEM((hp, tq, LANES), F32),
                        pltpu.VMEM((hp, tq, LANES), F32)],
        compiler_params=_params(("arbitrary", "arbitrary", "arbitrary")),
        name="flash",
    )(q, k, v)


def _sb_kernel(q_ref, k_ref, v_ref, tri_ref, o_ref, carry_ref, acc_ref, *, tq, tk):
    qi = pl.program_id(2)
    carry_ref[...] = jnp.zeros(carry_ref.shape, F32)
    acc_ref[...] = jnp.zeros(acc_ref.shape, F32)
    lo = lax.broadcasted_iota(I32, (tq, LANES), 1) < HALF
    qpair = q_ref[...]
    zero = jnp.zeros_like(qpair)
    qh = (jnp.where(lo, qpair, zero), jnp.where(lo, zero, qpair))
    tri = tri_ref[...]
    nrep = tk // LANES

    def block(j, masked):
        koff = pl.multiple_of(j * tk, tk)
        k = k_ref[pl.ds(koff, tk), :]
        v = v_ref[pl.ds(koff, tk), :]
        if masked:
            row = qi * tq + lax.broadcasted_iota(I32, (tq, tk), 0)
            col = j * tk + lax.broadcasted_iota(I32, (tq, tk), 1)
            strict = row > col
        for i in range(2):
            z = lax.dot_general(qh[i], k, (((1,), (1,)), ((), ())), preferred_element_type=F32)
            sp = jnp.maximum(z, 0.0) + jnp.log1p(jnp.exp(-jnp.abs(z)))
            log_keep = -sp
            if masked:
                log_keep = jnp.where(strict, log_keep, 0.0)
            inner = jnp.dot(log_keep.astype(BF16), tri, preferred_element_type=F32)
            carry = carry_ref[i]
            w = jnp.exp((z - sp) + inner + _lane_tile(carry, nrep))
            if masked:
                w = jnp.where(strict, w, 0.0)
            acc_ref[i] = acc_ref[i] + jnp.dot(w.astype(BF16), v, preferred_element_type=F32)
            carry_ref[i] = carry + jnp.sum(log_keep, axis=-1, keepdims=True)

    ndiag = tq // tk
    nfull = qi * ndiag
    for d in reversed(range(ndiag)):
        block(nfull + d, True)

    def body(t, c):
        block(nfull - 1 - t, False)
        return c

    lax.fori_loop(0, nfull, body, 0)
    o_ref[...] = jnp.where(lo, acc_ref[0], acc_ref[1]).astype(o_ref.dtype)


def _stick_breaking(q, k, v, v_col, batch, seq, tq, tk):
    n = q.shape[0]
    groups = q.shape[1] // LANES
    nq = seq // tq
    vb = v_col // LANES
    tri = (np.arange(tk)[:, None] > np.arange(tk)[None, :]).astype(np.float32)
    return pl.pallas_call(
        functools.partial(_sb_kernel, tq=tq, tk=tk),
        grid=(batch, groups, nq),
        in_specs=[
            pl.BlockSpec((tq, LANES), lambda b, g, i: (b * nq + i, g)),
            pl.BlockSpec((seq, LANES), lambda b, g, i: (b, g)),
            pl.BlockSpec((seq, LANES), lambda b, g, i: (b, vb + g)),
            pl.BlockSpec((tk, tk), lambda b, g, i: (0, 0)),
        ],
        out_specs=pl.BlockSpec((tq, LANES), lambda b, g, i: (b * nq + i, g)),
        out_shape=SDS((n, groups * LANES), BF16),
        scratch_shapes=[pltpu.VMEM((2, tq, LANES), F32), pltpu.VMEM((2, tq, LANES), F32)],
        compiler_params=_params(("arbitrary", "arbitrary", "arbitrary")),
        name="stick_breaking",
    )(q, k, v, jnp.asarray(tri, BF16))


def _merge_kernel(oa_ref, ob_ref, oc_ref, wa_ref, wb_ref, wc_ref, ga_ref, gb_ref, gc_ref, o_ref):
    def branch(o_r, w_r, g_r):
        y = jnp.dot(o_r[...], w_r[...], preferred_element_type=F32)
        return jax.nn.sigmoid(g_r[...].astype(F32)) * y

    o_ref[...] = (branch(oa_ref, wa_ref, ga_ref) + branch(ob_ref, wb_ref, gb_ref)
                  + branch(oc_ref, wc_ref, gc_ref)).astype(BF16)


def _merge(oa, ob, oc, wa, wb, wc, pk, tm, tn):
    n = oa.shape[0]
    d = wa.shape[1]
    gate_specs = [pl.BlockSpec((tm, tn), functools.partial(lambda i, j, base: (i, base + j),
                                                            base=(COL_GATES + br * d) // tn))
                  for br in range(N_BRANCH)]
    return pl.pallas_call(
        _merge_kernel,
        grid=(n // tm, d // tn),
        in_specs=[
            pl.BlockSpec((tm, oa.shape[1]), lambda i, j: (i, 0)),
            pl.BlockSpec((tm, ob.shape[1]), lambda i, j: (i, 0)),
            pl.BlockSpec((tm, oc.shape[1]), lambda i, j: (i, 0)),
            pl.BlockSpec((wa.shape[0], tn), lambda i, j: (0, j)),
            pl.BlockSpec((wb.shape[0], tn), lambda i, j: (0, j)),
            pl.BlockSpec((wc.shape[0], tn), lambda i, j: (0, j)),
        ] + gate_specs,
        out_specs=pl.BlockSpec((tm, tn), lambda i, j: (i, j)),
        out_shape=SDS((n, d), BF16),
        compiler_params=_params(("arbitrary", "arbitrary")),
        name="merge",
    )(oa, ob, oc, wa, wb, wc, pk, pk, pk)


def _pack_rows(o_ref, y):
    t = y.shape[0]
    half = D_MODEL // 2
    for s in range(ROW_SLABS):
        a = y[:, s * LANES:(s + 1) * LANES].astype(BF16).astype(F32)
        b = y[:, half + s * LANES:half + (s + 1) * LANES].astype(BF16).astype(F32)
        u = lax.bitcast_convert_type(a, U32) | (lax.bitcast_convert_type(b, U32) >> 16)
        o_ref[pl.ds(s, t, stride=ROW_SLABS), :] = u


def _unpack_rows(ref, t):
    first, second = [], []
    for s in range(ROW_SLABS):
        u = ref[pl.ds(s, t, stride=ROW_SLABS), :]
        first.append(lax.bitcast_convert_type(u & jnp.uint32(0xFFFF0000), F32))
        second.append(lax.bitcast_convert_type(u << 16, F32))
    return first, second


def _outproj_route_kernel(mg_ref, x_ref, wo_ref, g_ref, wr_hi_ref, wr_lo_ref, tri_ref,
                          xo_ref, h_ref, meta_ref, cnt_ref, base_ref):
    @pl.when(pl.program_id(0) == 0)
    def _():
        base_ref[...] = jnp.zeros_like(base_ref)

    tm = x_ref.shape[0]
    y = x_ref[...] + jnp.dot(mg_ref[...], wo_ref[...], preferred_element_type=F32)
    xo_ref[...] = y
    h = _rms_rows(y, g_ref[...])
    _pack_rows(h_ref, h)

    h_hi, h_lo = _split2(h)
    w_hi = wr_hi_ref[...]
    logits = (jnp.dot(h_hi, w_hi, preferred_element_type=F32) + jnp.dot(h_lo, w_hi, preferred_element_type=F32)
              + jnp.dot(h_hi, wr_lo_ref[...], preferred_element_type=F32))

    lane = lax.broadcasted_iota(I32, (tm, LANES), 1)
    far = jnp.int32(4 * LANES)

    def rmax(x):
        return jnp.max(x, axis=-1, keepdims=True)

    def rmin(x):
        return jnp.min(x, axis=-1, keepdims=True)

    def rsum(x):
        return jnp.sum(x, axis=-1, keepdims=True)

    is_group = lane < N_GROUPS
    gl = jnp.where(is_group, logits, NEG_BIG)
    gmax = rmax(gl)
    g_idx = rmin(jnp.where(is_group & (gl == gmax), lane, far))
    p_group = 1.0 / rsum(jnp.where(is_group, jnp.exp(gl - gmax), 0.0))
    e_lane = lane - N_GROUPS
    valid = (e_lane >= 0) & (e_lane < N_EXPERTS) & ((e_lane >> 3) == g_idx)
    em = jnp.where(valid, logits, NEG_BIG)
    v1 = rmax(em)
    i1 = rmin(jnp.where(valid & (em == v1), lane, far))
    valid2 = valid & (lane != i1)
    em2 = jnp.where(valid2, logits, NEG_BIG)
    v2 = rmax(em2)
    i2 = rmin(jnp.where(valid2 & (em2 == v2), lane, far))
    t2 = jnp.exp(v2 - v1)
    w1 = p_group / (1.0 + t2)
    w2 = p_group * t2 / (1.0 + t2)

    sel1 = lane == i1
    sel2 = lane == i2
    onehot = jnp.where(sel1 | sel2, 1.0, 0.0)
    before = jnp.dot(tri_ref[...], onehot.astype(BF16), preferred_element_type=F32) + base_ref[0:1, :]
    rank1 = rsum(jnp.where(sel1, before, 0.0))
    rank2 = rsum(jnp.where(sel2, before, 0.0))
    total = base_ref[0:1, :] + jnp.sum(onehot, axis=0, keepdims=True)
    base_ref[...] = jnp.broadcast_to(total, base_ref.shape)
    cnt_ref[...] = jnp.broadcast_to(total, cnt_ref.shape)

    e1 = (i1 - N_GROUPS).astype(F32)
    e2 = (i2 - N_GROUPS).astype(F32)
    meta = jnp.zeros((tm, LANES), F32)
    for col, val in enumerate((e1, e2, rank1, rank2, w1, w2)):
        meta = jnp.where(lane == col, val, meta)
    meta_ref[...] = meta


def _outproj_route(mg, x2, wo, g, wr_hi, wr_lo, tm):
    n, d = x2.shape
    tri = (np.arange(tm)[:, None] > np.arange(tm)[None, :]).astype(np.float32)
    return pl.pallas_call(
        _outproj_route_kernel,
        grid=(n // tm,),
        in_specs=[
            pl.BlockSpec((tm, d), lambda i: (i, 0)),
            pl.BlockSpec((tm, d), lambda i: (i, 0)),
            pl.BlockSpec((d, d), lambda i: (0, 0)),
            pl.BlockSpec((1, d), lambda i: (0, 0)),
            pl.BlockSpec((d, LANES), lambda i: (0, 0)),
            pl.BlockSpec((d, LANES), lambda i: (0, 0)),
            pl.BlockSpec((tm, tm), lambda i: (0, 0)),
        ],
        out_specs=[
            pl.BlockSpec((tm, d), lambda i: (i, 0)),
            pl.BlockSpec((tm * ROW_SLABS, LANES), lambda i: (i, 0)),
            pl.BlockSpec((tm, LANES), lambda i: (i, 0)),
            pl.BlockSpec((SUBLANES, LANES), lambda i: (0, 0)),
        ],
        out_shape=[SDS((n, d), F32), SDS((n * ROW_SLABS, LANES), U32), SDS((n, LANES), F32),
                   SDS((SUBLANES, LANES), F32)],
        scratch_shapes=[pltpu.VMEM((SUBLANES, LANES), F32)],
        compiler_params=_params(("arbitrary",)),
        name="outproj_route",
    )(mg, x2, wo, g, wr_hi, wr_lo, jnp.asarray(tri, BF16))


def _row_copy(src_ref, src_row, dst_ref, dst_row, sem):
    return pltpu.make_async_copy(
        src_ref.at[pl.ds(pl.multiple_of(src_row * ROW_SLABS, ROW_SLABS), ROW_SLABS), :],
        dst_ref.at[pl.ds(pl.multiple_of(dst_row * ROW_SLABS, ROW_SLABS), ROW_SLABS), :], sem)


def _dispatch_kernel(slot_ref, h_ref, xs_in_ref, xs_ref, sem):
    del xs_in_ref
    tm = h_ref.shape[0] // ROW_SLABS

    def issue(t, c):
        _row_copy(h_ref, t, xs_ref, slot_ref[0, 0, t], sem).start()
        _row_copy(h_ref, t, xs_ref, slot_ref[0, 1, t], sem).start()
        return c

    lax.fori_loop(0, tm, issue, 0)

    def drain(t, c):
        _row_copy(h_ref, 0, xs_ref, 0, sem).wait()
        _row_copy(h_ref, 0, xs_ref, 0, sem).wait()
        return c

    lax.fori_loop(0, tm, drain, 0)


def _dispatch(slots, h_rows, xs_init, tm):
    n = h_rows.shape[0] // ROW_SLABS
    return pl.pallas_call(
        _dispatch_kernel,
        grid=(n // tm,),
        in_specs=[
            pl.BlockSpec((1, TOP_K, tm), lambda i: (i, 0, 0), memory_space=pltpu.SMEM),
            pl.BlockSpec((tm * ROW_SLABS, LANES), lambda i: (i, 0)),
            pl.BlockSpec(memory_space=pl.ANY),
        ],
        out_specs=pl.BlockSpec(memory_space=pl.ANY),
        out_shape=SDS(xs_init.shape, U32),
        scratch_shapes=[pltpu.SemaphoreType.DMA(())],
        input_output_aliases={2: 0},
        compiler_params=_params(("arbitrary",)),
        name="dispatch",
    )(slots, h_rows, xs_init)


def _experts_kernel(be_ref, nv_ref, xs_ref, wg_ref, wu_ref, wd_ref, ys_ref, wg_s, wu_s, wd_s):
    b = pl.program_id(0)
    prev = be_ref[jnp.maximum(b - 1, 0)]

    @pl.when((b == 0) | (be_ref[b] != prev))
    def _():
        wg_s[...] = wg_ref[0].astype(BF16)
        wu_s[...] = wu_ref[0].astype(BF16)
        wd_s[...] = wd_ref[0].astype(BF16)

    blk = xs_ref.shape[0] // ROW_SLABS

    @pl.when(b < nv_ref[0])
    def _():
        first, second = _unpack_rows(xs_ref, blk)
        x = jnp.concatenate(first + second, axis=1).astype(BF16)
        g = jnp.dot(x, wg_s[...], preferred_element_type=F32)
        u = jnp.dot(x, wu_s[...], preferred_element_type=F32)
        a = (g * jax.nn.sigmoid(g) * u).astype(BF16)
        _pack_rows(ys_ref, jnp.dot(a, wd_s[...], preferred_element_type=F32))

    @pl.when(b >= nv_ref[0])
    def _():
        ys_ref[...] = jnp.zeros(ys_ref.shape, U32)


def _experts(block_e, nvalid, xs, wg, wu, wd):
    nb = block_e.shape[0]
    rows = MOE_BLOCK * ROW_SLABS
    d, de = wg.shape[1], wg.shape[2]
    grid_spec = pltpu.PrefetchScalarGridSpec(
        num_scalar_prefetch=2,
        grid=(nb,),
        in_specs=[
            pl.BlockSpec((rows, LANES), lambda b, be, nv: (b, 0)),
            pl.BlockSpec((1, d, de), lambda b, be, nv: (be[b], 0, 0)),
            pl.BlockSpec((1, d, de), lambda b, be, nv: (be[b], 0, 0)),
            pl.BlockSpec((1, de, d), lambda b, be, nv: (be[b], 0, 0)),
        ],
        out_specs=pl.BlockSpec((rows, LANES), lambda b, be, nv: (b, 0)),
        scratch_shapes=[pltpu.VMEM((d, de), BF16), pltpu.VMEM((d, de), BF16), pltpu.VMEM((de, d), BF16)],
    )
    return pl.pallas_call(
        _experts_kernel,
        grid_spec=grid_spec,
        out_shape=SDS(xs.shape, U32),
        compiler_params=_params(("arbitrary",)),
        name="experts",
    )(block_e, nvalid, xs, wg, wu, wd)


def _combine_kernel(slot_ref, x_ref, meta_ref, ys_ref, o_ref, buf_ref, sem):
    tm = x_ref.shape[0]

    def issue(t, c):
        _row_copy(ys_ref, slot_ref[0, 0, t], buf_ref.at[0], t, sem).start()
        _row_copy(ys_ref, slot_ref[0, 1, t], buf_ref.at[1], t, sem).start()
        return c

    lax.fori_loop(0, tm, issue, 0)

    def drain(t, c):
        _row_copy(ys_ref, 0, buf_ref.at[0], 0, sem).wait()
        _row_copy(ys_ref, 0, buf_ref.at[1], 0, sem).wait()
        return c

    lax.fori_loop(0, tm, drain, 0)

    meta = meta_ref[...]
    w1 = meta[:, 4:5]
    w2 = meta[:, 5:6]
    f1, s1 = _unpack_rows(buf_ref.at[0], tm)
    f2, s2 = _unpack_rows(buf_ref.at[1], tm)
    half = D_MODEL // 2
    for s in range(ROW_SLABS):
        c0 = s * LANES
        o_ref[:, c0:c0 + LANES] = x_ref[:, c0:c0 + LANES] + (w1 * f1[s] + w2 * f2[s])
        c1 = half + s * LANES
        o_ref[:, c1:c1 + LANES] = x_ref[:, c1:c1 + LANES] + (w1 * s1[s] + w2 * s2[s])


def _combine(slots, x2, meta, ys, tm):
    n, d = x2.shape
    return pl.pallas_call(
        _combine_kernel,
        grid=(n // tm,),
        in_specs=[
            pl.BlockSpec((1, TOP_K, tm), lambda i: (i, 0, 0), memory_space=pltpu.SMEM),
            pl.BlockSpec((tm, d), lambda i: (i, 0)),
            pl.BlockSpec((tm, LANES), lambda i: (i, 0)),
            pl.BlockSpec(memory_space=pl.ANY),
        ],
        out_specs=pl.BlockSpec((tm, d), lambda i: (i, 0)),
        out_shape=SDS((n, d), F32),
        scratch_shapes=[pltpu.VMEM((TOP_K, tm * ROW_SLABS, LANES), U32), pltpu.SemaphoreType.DMA(())],
        compiler_params=_params(("arbitrary",)),
        name="combine",
    )(slots, x2, meta, ys)


def _swap_halves(w):
    h = w.shape[-1] // 2
    return jnp.concatenate([w[..., h:], w[..., :h]], axis=-1)


def _pack_w_in(w):
    d = w.shape[0]
    o = 0
    seg = {}
    for name, width in (("cq", Q_LORA), ("ckv", KV_LORA), ("kr", MLA_ROPE), ("sbq", 1024), ("sbk", 1024),
                        ("sbv", 1024), ("fq", 1024), ("fk", 1024), ("fv", 1024), ("ff", FOX_HEADS),
                        ("gates", N_BRANCH * D_MODEL)):
        seg[name] = w[:, o:o + width]
        o += width
    kr = seg["kr"]
    krs = _swap_halves(kr)
    special = jnp.concatenate(
        [kr, kr, krs, krs, seg["ff"], jnp.zeros((d, LANES - FOX_HEADS), w.dtype), jnp.zeros((d, 2 * LANES), w.dtype)],
        axis=1)
    packed = jnp.concatenate([seg["cq"], seg["ckv"], seg["sbq"], seg["sbk"], seg["sbv"], seg["fq"], seg["fk"],
                              seg["fv"], seg["gates"], special], axis=1)
    return packed.astype(BF16)


def _pack_w_uq(w):
    w3 = w.reshape(w.shape[0], MLA_HEADS, MLA_QK)
    nope = w3[:, :, :MLA_NOPE]
    rope = w3[:, :, MLA_NOPE:]
    rsw = _swap_halves(rope)
    pairs = []
    for p in range(MLA_HEADS // 2):
        h0, h1 = 2 * p, 2 * p + 1
        pairs.append(jnp.concatenate([nope[:, h0], nope[:, h1], rope[:, h0], rope[:, h1], rsw[:, h0], rsw[:, h1]],
                                     axis=1))
    return jnp.stack(pairs, axis=0).astype(BF16)


def _mla_gains(gq, gk):
    def rows(g):
        rope = g[MLA_NOPE:]
        return [g[:MLA_NOPE], jnp.concatenate([rope, rope]), jnp.concatenate([_swap_halves(rope)] * 2)]

    z = jnp.zeros((LANES,), F32)
    return jnp.stack(rows(gq) + rows(gk) + [z, z], axis=0).astype(F32)


def _rope_tables(positions):
    inv_freq = ROPE_THETA ** (-jnp.arange(0, MLA_ROPE, 2, dtype=F32) / MLA_ROPE)
    ang = positions.astype(F32).reshape(-1)[:, None] * inv_freq
    c, s = jnp.cos(ang), jnp.sin(ang)
    return jnp.concatenate([c, c, c, c], axis=1), jnp.concatenate([-s, s, -s, s], axis=1)


def _block_tables(counts, n_blocks):
    padded = ((counts + MOE_BLOCK - 1) // MOE_BLOCK) * MOE_BLOCK
    pad_ends = jnp.cumsum(padded)
    pad_starts = pad_ends - padded
    starts = jnp.arange(n_blocks, dtype=I32) * MOE_BLOCK
    be = jnp.minimum(jnp.searchsorted(pad_ends, starts, side="right"), N_EXPERTS - 1).astype(I32)
    nvalid = (pad_ends[-1] // MOE_BLOCK).astype(I32)
    last = be[jnp.maximum(nvalid - 1, 0)]
    be = jnp.where(jnp.arange(n_blocks) < nvalid, be, last)
    return pad_starts, be, nvalid.reshape(1)


def _tile(n, want):
    t = min(n, want)
    assert n % t == 0
    return t


def kernel(x, positions, norm_mix, w_in, mla_cq_norm, mla_w_uq, mla_ckv_norm, mla_w_ukv, mla_q_norm, mla_k_norm,
           sb_q_norm, sb_k_norm, fox_q_norm, fox_k_norm, fox_f_bias, w_branch_mla, w_branch_sb, w_branch_fox,
           w_out, norm_ffn, w_group, w_router, w_e_gate, w_e_up, w_e_down):
    batch, seq, d = x.shape
    n = batch * seq
    depth = w_in.shape[0]
    assert d == D_MODEL
    x2 = x.reshape(n, d).astype(F32)
    c2, s2 = _rope_tables(positions)

    tm_big = _tile(n, 1024)
    tm_mid = _tile(n, 512)
    tq = _tile(seq, 512)
    tk_sb = _tile(seq, 256)
    n_assign = n * TOP_K
    n_blocks = (n_assign + N_EXPERTS * (MOE_BLOCK - 1)) // MOE_BLOCK + 1
    xs_init = jnp.zeros((n_blocks * MOE_BLOCK * ROW_SLABS, LANES), U32)
    row1 = lambda v: v.reshape(1, -1).astype(F32)

    for l in range(depth):
        pk, fsp = _inproj(x2, row1(norm_mix[l]), _pack_w_in(w_in[l]), tm_big, 512)

        q_a, k_a, v_a = _mla_prep(pk, fsp, row1(mla_cq_norm[l]), row1(mla_ckv_norm[l]), _pack_w_uq(mla_w_uq[l]),
                                  mla_w_ukv[l].astype(BF16), _mla_gains(mla_q_norm[l], mla_k_norm[l]), c2, s2, tm_mid)
        o_a = _flash(q_a, k_a, v_a, 0, batch, seq, 1, 2 * LANES, tq, tq)

        sb_scale = SB_DIM ** -0.5
        q_b = _head_norm(pk, COL_SB_Q, row1(jnp.tile(sb_q_norm[l], SB_HEADS) * sb_scale), tm_mid)
        k_b = _head_norm(pk, COL_SB_K, row1(jnp.tile(sb_k_norm[l], SB_HEADS)), tm_mid)
        o_b = _stick_breaking(q_b, k_b, pk, COL_SB_V, batch, seq, tq, tk_sb)

        bias_row = jnp.zeros((1, LANES), F32).at[0, :FOX_HEADS].set(fox_f_bias[l].astype(F32))
        cf = _fox_cumsum(fsp, bias_row, batch, seq, tq)
        fox_scale = FOX_DIM ** -0.5
        q_c = _head_norm(pk, COL_FOX_Q, row1(jnp.tile(fox_q_norm[l], FOX_HEADS) * fox_scale), tm_mid, cf, True)
        k_c = _head_norm(pk, COL_FOX_K, row1(jnp.tile(fox_k_norm[l], FOX_HEADS)), tm_mid, cf, False)
        o_c = _flash(q_c, k_c, pk, COL_FOX_V, batch, seq, 2, LANES, tq, tq)

        merged = _merge(o_a, o_b, o_c, w_branch_mla[l].astype(BF16), w_branch_sb[l].astype(BF16),
                        w_branch_fox[l].astype(BF16), pk, tm_big, 512)

        w_route = jnp.concatenate([w_group[l], w_router[l],
                                   jnp.zeros((d, LANES - N_GROUPS - N_EXPERTS), F32)], axis=1).astype(F32)
        wr_hi = w_route.astype(BF16)
        wr_lo = (w_route - wr_hi.astype(F32)).astype(BF16)
        x_mid, h_rows, meta, counts = _outproj_route(merged, x2, w_out[l].astype(BF16), row1(norm_ffn[l]),
                                                     wr_hi, wr_lo, tm_mid)
        cnt = counts[0, N_GROUPS:N_GROUPS + N_EXPERTS].astype(I32)
        pad_starts, block_e, nvalid = _block_tables(cnt, n_blocks)
        experts = meta[:, 0:TOP_K].astype(I32)
        slots = pad_starts[experts] + meta[:, TOP_K:2 * TOP_K].astype(I32)
        h3 = h_rows.reshape(n, ROW_SLABS, LANES)
        xs3 = xs_init.reshape(-1, ROW_SLABS, LANES).at[slots[:, 0]].set(h3).at[slots[:, 1]].set(h3)
        ys = _experts(block_e, nvalid, xs3.reshape(-1, LANES), w_e_gate[l], w_e_up[l], w_e_down[l])
        ys3 = ys.reshape(-1, ROW_SLABS, LANES)

        def unpack(u):
            a = lax.bitcast_convert_type(u & jnp.uint32(0xFFFF0000), F32).reshape(n, d // 2)
            b = lax.bitcast_convert_type(u << 16, F32).reshape(n, d // 2)
            return jnp.concatenate([a, b], axis=1)

        x2 = x_mid + meta[:, 4:5] * unpack(ys3[slots[:, 0]]) + meta[:, 5:6] * unpack(ys3[slots[:, 1]])

    return x2.reshape(batch, seq, d).astype(x.dtype)
```
